```python
import jax, jax.numpy as jnp
from jax import lax
import numpy as np

D_MODEL = 1024
BATCH = 2
SEQ = 16384
DEPTH = 4
DEC_BATCH = 16
DEC_SEQ = 32
PAST_LEN = 2048

CHUNK = 64
HG_BLOCK = 16
Q_BLOCK = 128
KEY_BLOCK = 128
EPS = 1e-6
N_BRANCH = 3
BRANCH_WIDTH = D_MODEL // 2
ML_HEADS = 4
ML_DK = BRANCH_WIDTH // ML_HEADS
ML_DV = BRANCH_WIDTH // ML_HEADS
ML_QK = ML_HEADS * ML_DK
ML_WIDTH = ML_HEADS * ML_DV
CONV_W = 4
SB_HEADS = 4
SB_DH = BRANCH_WIDTH // SB_HEADS
SB_WIDTH = SB_HEADS * SB_DH
HG_HEADS = 4
HG_DK = BRANCH_WIDTH // HG_HEADS
HG_DV = BRANCH_WIDTH // HG_HEADS
HG_QK = HG_HEADS * HG_DK
HG_WIDTH = HG_HEADS * HG_DV
D_FF = 2 * D_MODEL
IN_SPLITS = (2 * ML_QK, ML_WIDTH, ML_WIDTH, ML_HEADS, ML_HEADS, SB_WIDTH, SB_WIDTH, SB_WIDTH, HG_QK, HG_QK, HG_WIDTH, HG_WIDTH)
P_IN = 2 * ML_QK + 2 * ML_WIDTH + 2 * ML_HEADS + 3 * SB_WIDTH + 2 * HG_QK + 2 * HG_WIDTH

kernel_name = 'hybrid_streaming_mlstm_stickbreak_hgrn2_step'


def _rms(x, w):
    xf = x.astype(jnp.float32)
    return (xf * lax.rsqrt(jnp.mean(xf * xf, axis=-1, keepdims=True) + EPS) * w).astype(x.dtype)


def _swiglu(h, w_gate, w_up, w_down):
    return (jax.nn.silu(h @ w_gate) * (h @ w_up)) @ w_down


def _causal_conv(x, prev, w, b):
    L = x.shape[1]
    xp = jnp.concatenate([prev.astype(x.dtype), x], axis=1)
    y = xp[:, 0:L] * w[0]
    for j in range(1, CONV_W):
        y = y + xp[:, j:j + L] * w[j]
    return y + b, xp[:, xp.shape[1] - (CONV_W - 1):]


def _scan_chunks(step, carry, xs, blk):
    L = xs[0].shape[1]
    if L <= blk or L % blk != 0:
        return step(carry, xs)
    nc = L // blk
    xs_c = tuple(jnp.moveaxis(a.reshape(a.shape[0], nc, blk, *a.shape[2:]), 1, 0) for a in xs)
    carry, ys = lax.scan(step, carry, xs_c)
    ys = jnp.moveaxis(ys, 0, 1)
    return carry, ys.reshape(ys.shape[0], L, *ys.shape[3:])


def _mlstm_chunk(carry, inp):
    C, n, m = carry
    q, k, v, ig, lf = inp
    L = q.shape[1]
    mask = jnp.tril(jnp.ones((L, L), dtype=bool))[None, :, :, None]
    b = jnp.cumsum(lf, axis=1)
    d = jnp.where(mask, b[:, :, None] - b[:, None] + ig[:, None], -jnp.inf)
    inter = b + m[:, None]
    m_t = jnp.maximum(inter, jnp.max(d, axis=2))
    w = jnp.exp(d - m_t[:, :, None]) * jnp.einsum('bthd,bshd->btsh', q, k)
    w_inter = jnp.exp(inter - m_t)
    num = jnp.einsum('btsh,bshv->bthv', w, v) + w_inter[..., None] * jnp.einsum('bthd,bhdv->bthv', q, C)
    den = jnp.sum(w, axis=2) + w_inter * jnp.einsum('bthd,bhd->bth', q, n)
    h = num / jnp.maximum(jnp.abs(den), jnp.exp(-m_t))[..., None]
    m_new = m_t[:, -1]
    w_end = jnp.exp(b[:, -1:] - b + ig - m_new[:, None])
    decay = jnp.exp(b[:, -1] + m - m_new)
    C_new = decay[..., None, None] * C + jnp.einsum('bsh,bshd,bshv->bhdv', w_end, k, v)
    n_new = decay[..., None] * n + jnp.einsum('bsh,bshd->bhd', w_end, k)
    return (C_new, n_new, m_new), h


def _hgrn_chunk(S, inp):
    q, lf, k, v = inp
    L = q.shape[1]
    mask = jnp.tril(jnp.ones((L, L), dtype=bool))[None, :, :, None, None]
    a = jnp.cumsum(lf, axis=1)
    dec = jnp.exp(jnp.where(mask, a[:, :, None] - a[:, None], -jnp.inf))
    scores = jnp.einsum('bthc,btshc,bshc->btsh', q, dec, k)
    o = jnp.einsum('btsh,bshv->bthv', scores, v) + jnp.einsum('bthc,bhcv->bthv', q * jnp.exp(a), S)
    a_end = a[:, -1]
    S_new = jnp.exp(a_end)[..., None] * S + jnp.einsum('bshc,bshv->bhcv', k * jnp.exp(a_end[:, None] - a), v)
    return S_new, o


def _sb_block(q_blk, qpos, k, v):
    B, Tk = k.shape[0], k.shape[1]
    nkb = -(-Tk // KEY_BLOCK)
    pad = nkb * KEY_BLOCK - Tk
    k = jnp.pad(k, ((0, 0), (0, pad), (0, 0), (0, 0)))
    v = jnp.pad(v, ((0, 0), (0, pad), (0, 0), (0, 0)))
    Tq, Tkp = q_blk.shape[1], nkb * KEY_BLOCK
    kpos = jnp.arange(Tkp)
    z = jnp.einsum('bqhd,bkhd->bhqk', q_blk, k).astype(jnp.float32) * SB_DH ** -0.5
    causal = kpos[None, :] < qpos[:, None]
    ls = jnp.where(causal, jax.nn.log_sigmoid(-z), 0.0).reshape(B, SB_HEADS, Tq, nkb, KEY_BLOCK)
    ar = jnp.arange(KEY_BLOCK)
    upper_in = (ar[:, None] >= ar[None, :]).astype(jnp.float32)
    arb = jnp.arange(nkb)
    upper_bl = (arb[:, None] > arb[None, :]).astype(jnp.float32)
    r_in = jnp.einsum('bhqnk,kj->bhqnj', ls, upper_in, precision=lax.Precision.HIGHEST)
    tail = jnp.einsum('bhqm,mn->bhqn', jnp.sum(ls, axis=-1), upper_bl, precision=lax.Precision.HIGHEST)
    r = (r_in + tail[..., None]).reshape(B, SB_HEADS, Tq, Tkp)
    a = jnp.where(causal, jnp.exp(z + r), 0.0)
    return jnp.einsum('bhqk,bkhd->bqhd', a, v.astype(jnp.float32))


def _stick_breaking(q, k, v, past_len):
    L = q.shape[1]
    if L % Q_BLOCK == 0 and L > Q_BLOCK:
        outs = []
        for i in range(L // Q_BLOCK):
            end = past_len + (i + 1) * Q_BLOCK
            qpos = past_len + i * Q_BLOCK + jnp.arange(Q_BLOCK)
            outs.append(_sb_block(q[:, i * Q_BLOCK:(i + 1) * Q_BLOCK], qpos, k[:, :end], v[:, :end]))
        return jnp.concatenate(outs, axis=1)
    return _sb_block(q, past_len + jnp.arange(L), k, v)


def _empty_past(b, dtype):
    return (jnp.zeros((b, 0, SB_HEADS, SB_DH), dtype), jnp.zeros((b, 0, SB_HEADS, SB_DH), dtype),
            jnp.zeros((b, ML_HEADS, ML_DK, ML_DV), jnp.float32), jnp.zeros((b, ML_HEADS, ML_DK), jnp.float32),
            jnp.zeros((b, ML_HEADS), jnp.float32), jnp.zeros((b, CONV_W - 1, 2 * ML_QK), dtype),
            jnp.zeros((b, HG_HEADS, HG_DK, HG_DV), jnp.float32))


def _layer(x, c, past, lp):
    (k_past, v_past, C0, n0, m0, conv0, S0) = past
    (norm_w, w_ada, b_ada, w_ff_gate, w_ff_up, w_ff_down, w_in, ml_conv_w, ml_conv_b, ml_b_i, ml_b_f,
     ml_norm_w, sb_qn, sb_kn, hg_lb, hg_norm_w, w_branch, w_mg, b_mg, w_out) = lp
    B, L, D = x.shape
    f32 = jnp.float32
    ada = (jax.nn.silu(c) @ w_ada + b_ada).reshape(B, 3, 3, 1, D)

    h = _rms(x, norm_w[0]) * (1 + ada[:, 0, 1]) + ada[:, 0, 0]
    x = x + 0.5 * ada[:, 0, 2] * _swiglu(h, w_ff_gate[0], w_ff_up[0], w_ff_down[0])

    h = _rms(x, norm_w[1]) * (1 + ada[:, 1, 1]) + ada[:, 1, 0]
    proj = h @ w_in
    (ml_qk, ml_v, ml_o, ml_i, ml_f, sb_q, sb_k, sb_v, hg_q, hg_f, hg_i, hg_g) = jnp.split(
        proj, [int(s) for s in np.cumsum(IN_SPLITS)[:-1]], axis=-1)

    qk_c, conv_new = _causal_conv(ml_qk, conv0, ml_conv_w, ml_conv_b)
    qk_c = jax.nn.silu(qk_c.astype(f32))
    q = qk_c[..., :ML_QK].reshape(B, L, ML_HEADS, ML_DK)
    k = qk_c[..., ML_QK:].reshape(B, L, ML_HEADS, ML_DK) * ML_DK ** -0.5
    v = ml_v.astype(f32).reshape(B, L, ML_HEADS, ML_DV)
    ig = ml_i.astype(f32) + ml_b_i
    lf = jax.nn.log_sigmoid(ml_f.astype(f32) + ml_b_f)
    (C_new, n_new, m_new), hm = _scan_chunks(
        _mlstm_chunk, (C0.astype(f32), n0.astype(f32), m0.astype(f32)), (q, k, v, ig, lf), CHUNK)
    y_ml = _rms(hm, ml_norm_w.reshape(ML_HEADS, ML_DV)).reshape(B, L, ML_WIDTH) * jax.nn.sigmoid(ml_o.astype(f32))

    P = k_past.shape[1]
    q_sb = _rms(sb_q.reshape(B, L, SB_HEADS, SB_DH), sb_qn)
    k_sb = _rms(sb_k.reshape(B, L, SB_HEADS, SB_DH), sb_kn)
    v_sb = sb_v.reshape(B, L, SB_HEADS, SB_DH)
    y_sb = _stick_breaking(q_sb, jnp.concatenate([k_past.astype(k_sb.dtype), k_sb], axis=1),
                           jnp.concatenate([v_past.astype(v_sb.dtype), v_sb], axis=1), P).reshape(B, L, SB_WIDTH)

    lb = hg_lb.reshape(HG_HEADS, HG_DK)
    lf_hg = jnp.logaddexp(jnp.log(lb), jnp.log1p(-lb) + jax.nn.log_sigmoid(hg_f.astype(f32).reshape(B, L, HG_HEADS, HG_DK)))
    q_hg = hg_q.astype(f32).reshape(B, L, HG_HEADS, HG_DK)
    v_hg = hg_i.astype(f32).reshape(B, L, HG_HEADS, HG_DV)
    S_new, ho = _scan_chunks(_hgrn_chunk, S0.astype(f32), (q_hg, lf_hg, -jnp.expm1(lf_hg), v_hg), HG_BLOCK)
    y_hg = _rms(ho, hg_norm_w.reshape(HG_HEADS, HG_DV)).reshape(B, L, HG_WIDTH) * jax.nn.silu(hg_g.astype(f32))

    merged = None
    for i, y_b in enumerate((y_ml, y_sb, y_hg)):
        term = jax.nn.sigmoid(h @ w_mg[i] + b_mg[i]) * (y_b.astype(x.dtype) @ w_branch[i])
        merged = term if merged is None else merged + term
    x = x + ada[:, 1, 2] * (merged @ w_out)

    h = _rms(x, norm_w[2]) * (1 + ada[:, 2, 1]) + ada[:, 2, 0]
    x = x + 0.5 * ada[:, 2, 2] * _swiglu(h, w_ff_gate[1], w_ff_up[1], w_ff_down[1])
    return x, (k_sb, v_sb, C_new, n_new, m_new, conv_new, S_new)


def setup_inputs(seed: int = 0) -> dict:
    key = jax.random.key(seed)
    ks = iter(jax.random.split(key, 40))
    D = D_MODEL

    def nrm(shape, s=1.0):
        return s * jax.random.normal(next(ks), shape, jnp.float32)

    return {
        'x_prompt': nrm((BATCH, SEQ, D)),
        'x_sample': nrm((DEC_BATCH, DEC_SEQ, D)),
        'cache_sb_k': nrm((DEPTH, DEC_BATCH, PAST_LEN, SB_HEADS, SB_DH)),
        'cache_sb_v': nrm((DEPTH, DEC_BATCH, PAST_LEN, SB_HEADS, SB_DH)),
        'state_mlstm_C': nrm((DEPTH, DEC_BATCH, ML_HEADS, ML_DK, ML_DV), 0.1),
        'state_mlstm_n': nrm((DEPTH, DEC_BATCH, ML_HEADS, ML_DK), 0.1),
        'state_mlstm_m': nrm((DEPTH, DEC_BATCH, ML_HEADS)),
        'state_mlstm_conv': nrm((DEPTH, DEC_BATCH, CONV_W - 1, 2 * ML_QK)),
        'state_hgrn_S': nrm((DEPTH, DEC_BATCH, HG_HEADS, HG_DK, HG_DV), 0.5),
        'c_prompt': nrm((BATCH, D)),
        'c_sample': nrm((DEC_BATCH, D)),
        'norm_w': 1.0 + nrm((DEPTH, 3, D), 0.1),
        'w_ada': nrm((DEPTH, D, 9 * D), 0.5 * D ** -0.5),
        'b_ada': nrm((DEPTH, 9 * D), 0.02),
        'w_ff_gate': nrm((DEPTH, 2, D, D_FF), D ** -0.5),
        'w_ff_up': nrm((DEPTH, 2, D, D_FF), D ** -0.5),
        'w_ff_down': nrm((DEPTH, 2, D_FF, D), D_FF ** -0.5),
        'w_in': nrm((DEPTH, D, P_IN), D ** -0.5),
        'ml_conv_w': nrm((DEPTH, CONV_W, 2 * ML_QK), CONV_W ** -0.5),
        'ml_conv_b': nrm((DEPTH, 2 * ML_QK), 0.02),
        'ml_b_i': nrm((DEPTH, ML_HEADS), 0.1),
        'ml_b_f': 3.0 + 3.0 * jax.random.uniform(next(ks), (DEPTH, ML_HEADS), jnp.float32),
        'ml_norm_w': 1.0 + nrm((DEPTH, ML_WIDTH), 0.1),
        'sb_q_norm_w': 1.0 + nrm((DEPTH, SB_DH), 0.1),
        'sb_k_norm_w': 1.0 + nrm((DEPTH, SB_DH), 0.1),
        'hg_lb_logits': nrm((DEPTH, HG_QK), 0.5),
        'hg_norm_w': 1.0 + nrm((DEPTH, HG_WIDTH), 0.1),
        'w_branch': nrm((DEPTH, N_BRANCH, BRANCH_WIDTH, D), BRANCH_WIDTH ** -0.5),
        'w_merge_gate': nrm((DEPTH, N_BRANCH, D, D), D ** -0.5),
        'b_merge_gate': nrm((DEPTH, N_BRANCH, D), 0.02),
        'w_out': nrm((DEPTH, D, D), D ** -0.5),
    }


def reference(x_prompt, x_sample, cache_sb_k, cache_sb_v, state_mlstm_C, state_mlstm_n, state_mlstm_m,
              state_mlstm_conv, state_hgrn_S, c_prompt, c_sample, norm_w, w_ada, b_ada, w_ff_gate, w_ff_up,
              w_ff_down, w_in, ml_conv_w, ml_conv_b, ml_b_i, ml_b_f, ml_norm_w, sb_q_norm_w, sb_k_norm_w,
              hg_lb_logits, hg_norm_w, w_branch, w_merge_gate, b_merge_gate, w_out):
    lb_cum = jnp.cumsum(jax.nn.softmax(hg_lb_logits.astype(jnp.float32), axis=0), axis=0)
    lb_all = lb_cum - lb_cum[0:1]
    past_p = _empty_past(x_prompt.shape[0], x_prompt.dtype)
    xp, xs = x_prompt, x_sample
    new_p, new_s = [], []
    for l in range(DEPTH):
        lp = (norm_w[l], w_ada[l], b_ada[l], w_ff_gate[l], w_ff_up[l], w_ff_down[l], w_in[l], ml_conv_w[l],
              ml_conv_b[l], ml_b_i[l], ml_b_f[l], ml_norm_w[l], sb_q_norm_w[l], sb_k_norm_w[l], lb_all[l],
              hg_norm_w[l], w_branch[l], w_merge_gate[l], b_merge_gate[l], w_out[l])
        xp, st_p = _layer(xp, c_prompt, past_p, lp)
        past_s = (cache_sb_k[l], cache_sb_v[l], state_mlstm_C[l], state_mlstm_n[l], state_mlstm_m[l],
                  state_mlstm_conv[l], state_hgrn_S[l])
        xs, st_s = _layer(xs, c_sample, past_s, lp)
        new_p.append(st_p)
        new_s.append(st_s)

    def stk(states, i):
        return jnp.stack([s[i] for s in states], axis=0)

    return (xp, xs, stk(new_p, 0), stk(new_s, 0), stk(new_p, 1), stk(new_s, 1), stk(new_p, 2), stk(new_s, 2),
            stk(new_p, 3), stk(new_s, 3), stk(new_p, 4), stk(new_s, 4), stk(new_p, 5), stk(new_s, 5),
            stk(new_p, 6), stk(new_s, 6))
```

```python
import functools
import math

import jax
import jax.numpy as jnp
from jax import lax
from jax.experimental import pallas as pl
from jax.experimental.pallas import tpu as pltpu

F32 = jnp.float32
BF16 = jnp.bfloat16
EPS = 1e-6

N_HEADS = 4
HEAD_DIM = 128
WIDTH = N_HEADS * HEAD_DIM
CONV_W = 4
N_GATES = 2 * N_HEADS

LANES = 128
SUBLANES = 8
VMEM_LIMIT = 56 * 1024 * 1024

TOKEN_TILE = 512
ML_CHUNK = 128
HG_CHUNK = 128
HG_SUB = 16
SB_TILE = 256
EXP_ZERO = -104.0

NT = (((1,), (1,)), ((), ()))
TN = (((0,), (0,)), ((), ()))


def _params(*sem):
    return pltpu.CompilerParams(dimension_semantics=sem, vmem_limit_bytes=VMEM_LIMIT)


def _sigmoid(x):
    return 1.0 / (1.0 + jnp.exp(-x))


def _softplus(x):
    return jnp.maximum(x, 0.0) + jnp.log1p(jnp.exp(-jnp.abs(x)))


def _split3(x):
    hi = x.astype(BF16)
    r = x - hi.astype(F32)
    mid = r.astype(BF16)
    lo = (r - mid.astype(F32)).astype(BF16)
    return hi, mid, lo


def _dot(a, b):
    return jnp.dot(a, b, preferred_element_type=F32)


def _tri_left(tri, x):
    hi, mid, lo = _split3(x)
    return _dot(tri, hi) + _dot(tri, mid) + _dot(tri, lo)


def _tri_right(x, tri):
    hi, mid, lo = _split3(x)
    return _dot(hi, tri) + _dot(mid, tri) + _dot(lo, tri)


def _modnorm(x, nw, scale, shift):
    ms = jnp.mean(x * x, axis=-1, keepdims=True)
    return x * lax.rsqrt(ms + EPS) * nw * (1.0 + scale) + shift


def _head_rms(x, w):
    ms = jnp.mean(x * x, axis=-1, keepdims=True)
    return x * lax.rsqrt(ms + EPS) * w


def _tile_rows(B, L):
    if L >= TOKEN_TILE:
        assert L % TOKEN_TILE == 0
        return 1, TOKEN_TILE
    nb = min(B, max(1, TOKEN_TILE // (2 * L)))
    assert B % nb == 0 and L % SUBLANES == 0
    return nb, L


def _const_spec(shape):
    n = len(shape)
    return pl.BlockSpec(shape, lambda *_: (0,) * n)


def _ada_kernel(c_ref, w_ref, b_ref, o_ref):
    c = c_ref[...]
    s = (c * _sigmoid(c)).astype(BF16)
    o_ref[0] = _dot(s, w_ref[0].astype(BF16)) + b_ref[0]


def _ada(c, w_ada, b_ada):
    depth, D, N = w_ada.shape
    R = c.shape[0]
    tn = N // 8
    return pl.pallas_call(
        _ada_kernel,
        grid=(depth, N // tn),
        in_specs=[pl.BlockSpec((R, D), lambda l, j: (0, 0)),
                  pl.BlockSpec((1, D, tn), lambda l, j: (l, 0, j)),
                  pl.BlockSpec((1, 1, tn), lambda l, j: (l, 0, j))],
        out_specs=pl.BlockSpec((1, R, tn), lambda l, j: (l, 0, j)),
        out_shape=jax.ShapeDtypeStruct((depth, R, N), F32),
        compiler_params=_params("parallel", "parallel"),
        name="ada",
    )(c, w_ada, b_ada.reshape(depth, 1, N))


def _ffn_kernel(x_ref, sh_ref, sc_ref, g_ref, nw_ref, wg_ref, wu_ref, wd_ref, o_ref):
    x = x_ref[...]
    nb, rows, D = x.shape
    h = _modnorm(x, nw_ref[...], sc_ref[...], sh_ref[...]).reshape(nb * rows, D).astype(BF16)
    g = _dot(h, wg_ref[...])
    u = _dot(h, wu_ref[...])
    a = (g * _sigmoid(g) * u).astype(BF16)
    y = _dot(a, wd_ref[...]).reshape(nb, rows, D)
    o_ref[...] = x + 0.5 * g_ref[...] * y


def _ffn(x, shift, scale, gate, nw, wg, wu, wd):
    B, L, D = x.shape
    nb, rows = _tile_rows(B, L)
    xs = pl.BlockSpec((nb, rows, D), lambda b, t: (b, t, 0))
    ada = pl.BlockSpec((nb, 1, D), lambda b, t: (b, 0, 0))
    return pl.pallas_call(
        _ffn_kernel,
        grid=(B // nb, L // rows),
        in_specs=[xs, ada, ada, ada, _const_spec(nw.shape), _const_spec(wg.shape),
                  _const_spec(wu.shape), _const_spec(wd.shape)],
        out_specs=xs,
        out_shape=jax.ShapeDtypeStruct(x.shape, F32),
        compiler_params=_params("parallel", "parallel"),
        name="ffn",
    )(x, shift, scale, gate, nw, wg, wu, wd)


def _proj_kernel(x_ref, sh_ref, sc_ref, nw_ref, wm_ref, wgate_ref, wgate_t_ref, qn_ref, kn_ref,
                 h_ref, ml_ref, hg_ref, q_ref, k_ref, v_ref, kb_ref, vb_ref, g_ref, gt_ref):
    x = x_ref[...]
    nb, rows, D = x.shape
    h = _modnorm(x, nw_ref[...], sc_ref[...], sh_ref[...]).astype(BF16)
    h_ref[...] = h
    h2 = h.reshape(nb * rows, D)
    ml_w = ml_ref.shape[-1]
    hg_w = hg_ref.shape[-1]
    ml_ref[...] = _dot(h2, wm_ref[:, 0:ml_w]).reshape(nb, rows, ml_w)
    off = ml_w
    for hd in range(N_HEADS):
        lo = hd * HEAD_DIM
        q = _dot(h2, wm_ref[:, off + lo:off + lo + HEAD_DIM])
        k = _dot(h2, wm_ref[:, off + WIDTH + lo:off + WIDTH + lo + HEAD_DIM])
        q = _head_rms(q, qn_ref[...])
        k = _head_rms(k, kn_ref[...]).reshape(nb, rows, HEAD_DIM)
        q_ref[:, :, lo:lo + HEAD_DIM] = q.astype(BF16).reshape(nb, rows, HEAD_DIM)
        k_ref[:, :, lo:lo + HEAD_DIM] = k
        kb_ref[:, :, lo:lo + HEAD_DIM] = k.astype(BF16)
    off += 2 * WIDTH
    v = _dot(h2, wm_ref[:, off:off + WIDTH]).reshape(nb, rows, WIDTH)
    v_ref[...] = v
    vb_ref[...] = v.astype(BF16)
    off += WIDTH
    hg_ref[...] = _dot(h2, wm_ref[:, off:off + hg_w]).reshape(nb, rows, hg_w)
    g_ref[...] = _dot(h2, wgate_ref[...]).reshape(nb, rows, N_GATES)
    for b in range(nb):
        gt_ref[b] = lax.dot_general(wgate_t_ref[...], h[b], NT, preferred_element_type=F32)


def _proj(x, shift, scale, nw, w_main, w_gate, w_gate_t, qn, kn):
    B, L, D = x.shape
    nb, rows = _tile_rows(B, L)
    ml_w = 4 * WIDTH
    hg_w = 4 * WIDTH

    def tok(width):
        return pl.BlockSpec((nb, rows, width), lambda b, t: (b, t, 0))

    ada = pl.BlockSpec((nb, 1, D), lambda b, t: (b, 0, 0))
    out_shapes = [((B, L, D), BF16), ((B, L, ml_w), F32), ((B, L, hg_w), F32), ((B, L, WIDTH), BF16),
                  ((B, L, WIDTH), F32), ((B, L, WIDTH), F32), ((B, L, WIDTH), BF16), ((B, L, WIDTH), BF16),
                  ((B, L, N_GATES), F32)]
    return pl.pallas_call(
        _proj_kernel,
        grid=(B // nb, L // rows),
        in_specs=[tok(D), ada, ada, _const_spec(nw.shape), _const_spec(w_main.shape), _const_spec(w_gate.shape),
                  _const_spec(w_gate_t.shape), _const_spec(qn.shape), _const_spec(kn.shape)],
        out_specs=[tok(s[-1]) for s, _ in out_shapes]
        + [pl.BlockSpec((nb, N_GATES, rows), lambda b, t: (b, 0, t))],
        out_shape=[jax.ShapeDtypeStruct(s, d) for s, d in out_shapes]
        + [jax.ShapeDtypeStruct((B, N_GATES, L), F32)],
        compiler_params=_params("parallel", "parallel"),
        name="proj",
    )(x, shift, scale, nw, w_main, w_gate, w_gate_t, qn, kn)


def _mlstm_kernel(ml_ref, g_ref, gt_ref, conv0_ref, c0_ref, n0_ref, m0_ref, cw_ref, cb_ref, gb_row_ref,
                  gb_col_ref, nw_ref, tri_ref, tri_t_ref,
                  y_ref, c_ref, n_ref, m_ref, conv_ref, prev_ref):
    step = pl.program_id(1)
    ck = ml_ref.shape[1]
    QK2 = 2 * WIDTH

    @pl.when(step == 0)
    def _():
        c_ref[...] = c0_ref[...]
        n_ref[...] = n0_ref[...]
        m_ref[...] = m0_ref[...]
        prev_ref[...] = conv0_ref[0]

    xqk = ml_ref[0, :, 0:QK2]
    xfull = jnp.concatenate([prev_ref[...], xqk], axis=0)
    acc = xfull[SUBLANES:] * cw_ref[CONV_W - 1:CONV_W, :]
    for j in range(CONV_W - 1):
        shifted = pltpu.roll(xfull, CONV_W - 1 - j, axis=0)[SUBLANES:]
        acc = acc + shifted * cw_ref[j:j + 1, :]
    acc = acc + cb_ref[...]
    qk = acc * _sigmoid(acc)
    last = xqk[ck - SUBLANES:, :]
    prev_ref[...] = last
    conv_ref[0] = last

    g = g_ref[0] + gb_row_ref[...]
    gt = gt_ref[0] + gb_col_ref[...]
    b_col = _tri_left(tri_ref[...], -_softplus(-g))
    b_row = _tri_right(-_softplus(-gt), tri_t_ref[...])

    row = lax.broadcasted_iota(jnp.int32, (ck, ck), 0)
    col = lax.broadcasted_iota(jnp.int32, (ck, ck), 1)
    causal = col <= row
    scale = HEAD_DIM ** -0.5
    for hd in range(N_HEADS):
        lo = hd * HEAD_DIM
        q_f = qk[:, lo:lo + HEAD_DIM]
        k_f = qk[:, WIDTH + lo:WIDTH + lo + HEAD_DIM] * scale
        q_h = q_f.astype(BF16)
        k_h = k_f.astype(BF16)
        v_h = ml_ref[0, :, QK2 + lo:QK2 + lo + HEAD_DIM].astype(BF16)
        o_h = ml_ref[0, :, QK2 + WIDTH + lo:QK2 + WIDTH + lo + HEAD_DIM]
        bc = b_col[:, N_HEADS + hd:N_HEADS + hd + 1]
        br = b_row[N_HEADS + hd:N_HEADS + hd + 1, :]
        ig_r = gt[hd:hd + 1, :]
        ig_c = g[:, hd:hd + 1]
        m_prev = m_ref[0, 0:1, hd:hd + 1]
        C = c_ref[0, hd]
        n_row = n_ref[0, hd:hd + 1, :]

        d = jnp.where(causal, bc - br + ig_r, -jnp.inf)
        inter = bc + m_prev
        m_t = jnp.maximum(inter, jnp.max(d, axis=1, keepdims=True))
        w = jnp.exp(d - m_t) * lax.dot_general(q_h, k_h, NT, preferred_element_type=F32)
        w_inter = jnp.exp(inter - m_t)
        num = _dot(w.astype(BF16), v_h) + w_inter * _dot(q_h, C.astype(BF16))
        den = jnp.sum(w, axis=1, keepdims=True) + w_inter * jnp.sum(q_f * n_row, axis=1, keepdims=True)
        hm = num / jnp.maximum(jnp.abs(den), jnp.exp(-m_t))

        m_new = m_t[ck - 1:ck, :]
        b_last = bc[ck - 1:ck, :]
        w_end = jnp.exp(b_last - bc + ig_c - m_new)
        decay = jnp.exp(b_last + m_prev - m_new)
        kw = k_f * w_end
        c_ref[0, hd] = decay * C + lax.dot_general(kw.astype(BF16), v_h, TN, preferred_element_type=F32)
        n_ref[0, hd:hd + 1, :] = decay * n_row + jnp.sum(kw, axis=0, keepdims=True)
        m_ref[0, 0:1, hd:hd + 1] = m_new

        y = _head_rms(hm, nw_ref[:, lo:lo + HEAD_DIM]) * _sigmoid(o_h)
        y_ref[0, :, lo:lo + HEAD_DIM] = y.astype(BF16)


def _mlstm(ml, g, gt, conv0, c0, n0, m0, cw, cb, gb, nw):
    B, L, ml_w = ml.shape
    ck = min(L, ML_CHUNK)
    assert L % ck == 0 and ck % SUBLANES == 0
    tri = jnp.tril(jnp.ones((ck, ck), BF16))
    conv0p = jnp.pad(conv0, ((0, 0), (SUBLANES - (CONV_W - 1), 0), (0, 0)))

    def state(shape):
        n = len(shape)
        return pl.BlockSpec((1,) + shape[1:], lambda b, c: (b,) + (0,) * (n - 1))

    y, c_new, n_new, m_new, conv = pl.pallas_call(
        _mlstm_kernel,
        grid=(B, L // ck),
        in_specs=[pl.BlockSpec((1, ck, ml_w), lambda b, c: (b, c, 0)),
                  pl.BlockSpec((1, ck, N_GATES), lambda b, c: (b, c, 0)),
                  pl.BlockSpec((1, N_GATES, ck), lambda b, c: (b, 0, c)),
                  state(conv0p.shape), state(c0.shape), state(n0.shape), state(m0.shape),
                  _const_spec(cw.shape), _const_spec(cb.shape), _const_spec((1, N_GATES)),
                  _const_spec((N_GATES, 1)), _const_spec(nw.shape), _const_spec(tri.shape), _const_spec(tri.shape)],
        out_specs=[pl.BlockSpec((1, ck, WIDTH), lambda b, c: (b, c, 0)),
                   state(c0.shape), state(n0.shape), state(m0.shape), state(conv0p.shape)],
        out_shape=[jax.ShapeDtypeStruct((B, L, WIDTH), BF16), jax.ShapeDtypeStruct(c0.shape, F32),
                   jax.ShapeDtypeStruct(n0.shape, F32), jax.ShapeDtypeStruct(m0.shape, F32),
                   jax.ShapeDtypeStruct(conv0p.shape, F32)],
        scratch_shapes=[pltpu.VMEM((SUBLANES, 2 * WIDTH), F32)],
        compiler_params=_params("parallel", "arbitrary"),
        name="mlstm",
    )(ml, g, gt, conv0p, c0, n0, m0, cw, cb, gb.reshape(1, N_GATES), gb.reshape(N_GATES, 1), nw, tri, tri.T)
    return y, c_new, n_new, m_new, conv[:, SUBLANES - (CONV_W - 1):]


def _sb_tile(q, k, v, tail, acc, u, masked):
    z = lax.dot_general(q, k, NT, preferred_element_type=F32) * (HEAD_DIM ** -0.5)
    sp = _softplus(z)
    if masked:
        row = lax.broadcasted_iota(jnp.int32, z.shape, 0)
        col = lax.broadcasted_iota(jnp.int32, z.shape, 1)
        causal = col < row
        sp = jnp.where(causal, sp, 0.0)
    hi = sp.astype(BF16)
    lo = (sp - hi.astype(F32)).astype(BF16)
    cum = _dot(hi, u) + _dot(lo, u)
    a = jnp.exp(z + (tail - cum))
    if masked:
        a = jnp.where(causal, a, 0.0)
    acc = acc + _dot(a.astype(BF16), v)
    return tail - cum[:, 0:1], acc


def _sb_prompt_kernel(zb_ref, q_ref, k_ref, v_ref, u_ref, o_ref):
    qi = pl.program_id(2)
    tq = q_ref.shape[1]
    q = q_ref[0]
    u = u_ref[...]
    zb = zb_ref[0]

    def kv(j):
        start = pl.multiple_of(j * tq, tq)
        return k_ref[0, pl.ds(start, tq), :], v_ref[0, pl.ds(start, tq), :]

    k, v = kv(qi)
    tail, acc = _sb_tile(q, k, v, jnp.zeros((tq, 1), F32), jnp.zeros((tq, HEAD_DIM), F32), u, True)

    def cond(c):
        j, tmax, _, _ = c
        return jnp.logical_and(j >= 0, tmax + zb > EXP_ZERO)

    def body(c):
        j, _, tail, acc = c
        k, v = kv(j)
        tail, acc = _sb_tile(q, k, v, tail, acc, u, False)
        return j - 1, jnp.max(tail), tail, acc

    _, _, _, acc = lax.while_loop(cond, body, (qi - 1, jnp.max(tail), tail, acc))
    o_ref[0] = acc.astype(BF16)


def _sb_prompt(zb, q, kb, vb):
    B, L, _ = q.shape
    tq = SB_TILE
    assert L % tq == 0
    u = jnp.tril(jnp.ones((tq, tq), BF16))
    return pl.pallas_call(
        _sb_prompt_kernel,
        grid=(B, N_HEADS, L // tq),
        in_specs=[pl.BlockSpec(memory_space=pltpu.SMEM),
                  pl.BlockSpec((1, tq, HEAD_DIM), lambda b, h, i: (b, i, h)),
                  pl.BlockSpec((1, L, HEAD_DIM), lambda b, h, i: (b, 0, h)),
                  pl.BlockSpec((1, L, HEAD_DIM), lambda b, h, i: (b, 0, h)),
                  _const_spec(u.shape)],
        out_specs=pl.BlockSpec((1, tq, HEAD_DIM), lambda b, h, i: (b, i, h)),
        out_shape=jax.ShapeDtypeStruct((B, L, WIDTH), BF16),
        compiler_params=_params("parallel", "parallel", "arbitrary"),
        name="sb_prompt",
    )(zb, q, kb, vb, u)


def _sb_sample_kernel(q_ref, kn_ref, vn_ref, kp_ref, vp_ref, us_ref, u_ref, o_ref):
    tq = q_ref.shape[1]
    tk = u_ref.shape[0]
    P = kp_ref.shape[1]
    q = q_ref[0]
    tail, acc = _sb_tile(q, kn_ref[0], vn_ref[0], jnp.zeros((tq, 1), F32), jnp.zeros((tq, HEAD_DIM), F32),
                         us_ref[...], True)
    for j in reversed(range(P // tk)):
        k = kp_ref[0, j * tk:(j + 1) * tk, :].astype(BF16)
        v = vp_ref[0, j * tk:(j + 1) * tk, :].astype(BF16)
        tail, acc = _sb_tile(q, k, v, tail, acc, u_ref[...], False)
    o_ref[0] = acc.astype(BF16)


def _sb_sample(q, kb, vb, k_past, v_past):
    B, L, _ = q.shape
    P = k_past.shape[1]
    tk = SB_TILE
    assert P % tk == 0
    us = jnp.tril(jnp.ones((L, L), BF16))
    u = jnp.tril(jnp.ones((tk, tk), BF16))
    new = pl.BlockSpec((1, L, HEAD_DIM), lambda b, h: (b, 0, h))
    past = pl.BlockSpec((1, P, HEAD_DIM), lambda b, h: (b, 0, h))
    return pl.pallas_call(
        _sb_sample_kernel,
        grid=(B, N_HEADS),
        in_specs=[new, new, new, past, past, _const_spec(us.shape), _const_spec(u.shape)],
        out_specs=new,
        out_shape=jax.ShapeDtypeStruct((B, L, WIDTH), BF16),
        compiler_params=_params("parallel", "parallel"),
        name="sb_sample",
    )(q, kb, vb, k_past, v_past, us, u)


def _hgrn_kernel(hg_ref, s0_ref, lb_ref, nw_ref, tri_ref, y_ref, st_ref):
    step = pl.program_id(1)
    ck = hg_ref.shape[1]

    @pl.when(step == 0)
    def _():
        st_ref[...] = s0_ref[...]

    f_all = hg_ref[0, :, WIDTH:2 * WIDTH]
    lb = lb_ref[...]
    e = jnp.exp(-jnp.abs(f_all))
    r = 1.0 / (1.0 + e)
    pos = f_all >= 0.0
    sig = jnp.where(pos, r, e * r)
    nsig = jnp.where(pos, e * r, r)
    lf = jnp.log(lb + (1.0 - lb) * sig)
    kk_all = (1.0 - lb) * nsig
    a_all = _tri_left(tri_ref[...], lf)

    row = lax.broadcasted_iota(jnp.int32, (ck, HEAD_DIM), 0)
    srow = lax.broadcasted_iota(jnp.int32, (HG_SUB, ck), 0)
    scol = lax.broadcasted_iota(jnp.int32, (HG_SUB, ck), 1)
    for hd in range(N_HEADS):
        lo = hd * HEAD_DIM
        q = hg_ref[0, :, lo:lo + HEAD_DIM]
        v = hg_ref[0, :, 2 * WIDTH + lo:2 * WIDTH + lo + HEAD_DIM].astype(BF16)
        gate = hg_ref[0, :, 3 * WIDTH + lo:3 * WIDTH + lo + HEAD_DIM]
        a = a_all[:, lo:lo + HEAD_DIM]
        kk = kk_all[:, lo:lo + HEAD_DIM]
        st = st_ref[0, hd]

        blocks = []
        for i in range(ck // HG_SUB):
            r0 = i * HG_SUB
            ref = a[r0 - 1:r0, :] if i > 0 else jnp.zeros((1, HEAD_DIM), F32)
            qh = (q[r0:r0 + HG_SUB] * jnp.exp(a[r0:r0 + HG_SUB] - ref)).astype(BF16)
            kh = (kk * jnp.exp(jnp.where(row < r0 + HG_SUB, ref - a, -jnp.inf))).astype(BF16)
            sc = lax.dot_general(qh, kh, NT, preferred_element_type=F32)
            blocks.append(jnp.where(scol <= srow + r0, sc, 0.0))
        scores = jnp.concatenate(blocks, axis=0) if len(blocks) > 1 else blocks[0]
        qt = (q * jnp.exp(a)).astype(BF16)
        o = _dot(scores.astype(BF16), v) + lax.dot_general(qt, st.astype(BF16), NT, preferred_element_type=F32)

        a_end = a[ck - 1:ck, :]
        k_end = (kk * jnp.exp(a_end - a)).astype(BF16)
        st_ref[0, hd] = st * jnp.exp(a_end) + lax.dot_general(v, k_end, TN, preferred_element_type=F32)

        y = _head_rms(o, nw_ref[:, lo:lo + HEAD_DIM]) * (gate * _sigmoid(gate))
        y_ref[0, :, lo:lo + HEAD_DIM] = y.astype(BF16)


def _hgrn(hg, s0t, lb, nw):
    B, L, hg_w = hg.shape
    ck = min(L, HG_CHUNK)
    assert L % ck == 0 and ck % HG_SUB == 0
    tri = jnp.tril(jnp.ones((ck, ck), BF16))
    st_spec = pl.BlockSpec((1,) + s0t.shape[1:], lambda b, c: (b, 0, 0, 0))
    return pl.pallas_call(
        _hgrn_kernel,
        grid=(B, L // ck),
        in_specs=[pl.BlockSpec((1, ck, hg_w), lambda b, c: (b, c, 0)), st_spec,
                  _const_spec(lb.shape), _const_spec(nw.shape), _const_spec(tri.shape)],
        out_specs=[pl.BlockSpec((1, ck, WIDTH), lambda b, c: (b, c, 0)), st_spec],
        out_shape=[jax.ShapeDtypeStruct((B, L, WIDTH), BF16), jax.ShapeDtypeStruct(s0t.shape, F32)],
        compiler_params=_params("parallel", "arbitrary"),
        name="hgrn",
    )(hg, s0t, lb, nw, tri)


def _merge_kernel(x_ref, h_ref, y0_ref, y1_ref, y2_ref, g_ref, wmg_ref, bmg_ref, wbr_ref, wout_ref, o_ref):
    x = x_ref[...]
    nb, rows, D = x.shape
    h = h_ref[...].reshape(nb * rows, D)
    merged = None
    for i, y_ref in enumerate((y0_ref, y1_ref, y2_ref)):
        y = y_ref[...].reshape(nb * rows, WIDTH)
        term = _sigmoid(_dot(h, wmg_ref[i]) + bmg_ref[i]) * _dot(y, wbr_ref[i])
        merged = term if merged is None else merged + term
    out = _dot(merged.astype(BF16), wout_ref[...]).reshape(nb, rows, D)
    o_ref[...] = x + g_ref[...] * out


def _merge(x, h, y_ml, y_sb, y_hg, gate, w_mg, b_mg, w_br, w_out):
    B, L, D = x.shape
    nb, rows = _tile_rows(B, L)

    def tok(width):
        return pl.BlockSpec((nb, rows, width), lambda b, t: (b, t, 0))

    ada = pl.BlockSpec((nb, 1, D), lambda b, t: (b, 0, 0))
    return pl.pallas_call(
        _merge_kernel,
        grid=(B // nb, L // rows),
        in_specs=[tok(D), tok(D), tok(WIDTH), tok(WIDTH), tok(WIDTH), ada, _const_spec(w_mg.shape),
                  _const_spec(b_mg.shape), _const_spec(w_br.shape), _const_spec(w_out.shape)],
        out_specs=tok(D),
        out_shape=jax.ShapeDtypeStruct(x.shape, F32),
        compiler_params=_params("parallel", "parallel"),
        name="merge",
    )(x, h, y_ml, y_sb, y_hg, gate, w_mg, b_mg, w_br, w_out)


def _layer(x, ada, past, lp):
    k_past, v_past, c0, n0, m0, conv0, s0 = past
    B, L, D = x.shape

    def mod(sub, part):
        return ada[:, 3 * sub + part]

    x = _ffn(x, mod(0, 0), mod(0, 1), mod(0, 2), lp["norm_w"][0:1], *lp["ffn0"])
    h, ml, hg, q, k, v, kb, vb, g, gt = _proj(x, mod(1, 0), mod(1, 1), lp["norm_w"][1:2], lp["w_main"],
                                              lp["w_gate"], lp["w_gate_t"], lp["sb_qn"], lp["sb_kn"])
    y_ml, c_new, n_new, m_new, conv_new = _mlstm(ml, g, gt, conv0, c0, n0, m0.reshape(B, 1, N_HEADS),
                                                 lp["conv_w"], lp["conv_b"], lp["gate_b"], lp["ml_nw"])
    if k_past is None:
        y_sb = _sb_prompt(lp["zb"], q, kb, vb)
    else:
        P = k_past.shape[1]
        y_sb = _sb_sample(q, kb, vb, k_past.reshape(B, P, WIDTH), v_past.reshape(B, P, WIDTH))
    y_hg, st_new = _hgrn(hg, jnp.swapaxes(s0, -1, -2), lp["hg_lb"], lp["hg_nw"])
    x = _merge(x, h, y_ml, y_sb, y_hg, mod(1, 2), lp["w_mg"], lp["b_mg"], lp["w_br"], lp["w_out"])
    x = _ffn(x, mod(2, 0), mod(2, 1), mod(2, 2), lp["norm_w"][2:3], *lp["ffn1"])
    new = (k.reshape(B, L, N_HEADS, HEAD_DIM), v.reshape(B, L, N_HEADS, HEAD_DIM), c_new, n_new,
           m_new.reshape(B, N_HEADS), conv_new, jnp.swapaxes(st_new, -1, -2))
    return x, new


def _layer_params(l, lb_all, norm_w, w_ff_gate, w_ff_up, w_ff_down, w_in, ml_conv_w, ml_conv_b, ml_b_i, ml_b_f,
                  ml_norm_w, sb_q_norm_w, sb_k_norm_w, hg_norm_w, w_branch, w_merge_gate, b_merge_gate, w_out):
    W = WIDTH
    w = w_in[l]
    g0 = 4 * W
    g1 = g0 + N_GATES
    w_main = jnp.concatenate([w[:, :g0], w[:, g1:]], axis=1).astype(BF16)
    w_gate = w[:, g0:g1].astype(BF16)
    qn, kn = sb_q_norm_w[l], sb_k_norm_w[l]
    zb = math.sqrt(HEAD_DIM) * jnp.max(jnp.abs(qn)) * jnp.max(jnp.abs(kn)) * 1.02 + 1.0
    D = norm_w.shape[-1]
    return {
        "norm_w": norm_w[l],
        "ffn0": (w_ff_gate[l, 0].astype(BF16), w_ff_up[l, 0].astype(BF16), w_ff_down[l, 0].astype(BF16)),
        "ffn1": (w_ff_gate[l, 1].astype(BF16), w_ff_up[l, 1].astype(BF16), w_ff_down[l, 1].astype(BF16)),
        "w_main": w_main, "w_gate": w_gate, "w_gate_t": w_gate.T,
        "sb_qn": qn.reshape(1, HEAD_DIM), "sb_kn": kn.reshape(1, HEAD_DIM), "zb": zb.reshape(1).astype(F32),
        "conv_w": ml_conv_w[l], "conv_b": ml_conv_b[l].reshape(1, 2 * W),
        "gate_b": jnp.concatenate([ml_b_i[l], ml_b_f[l]]), "ml_nw": ml_norm_w[l].reshape(1, W),
        "hg_lb": lb_all[l].reshape(1, W), "hg_nw": hg_norm_w[l].reshape(1, W),
        "w_mg": w_merge_gate[l].astype(BF16), "b_mg": b_merge_gate[l].reshape(3, 1, D),
        "w_br": w_branch[l].astype(BF16), "w_out": w_out[l].astype(BF16),
    }


def kernel(x_prompt, x_sample, cache_sb_k, cache_sb_v, state_mlstm_C, state_mlstm_n, state_mlstm_m, state_mlstm_conv, state_hgrn_S, c_prompt, c_sample, norm_w, w_ada, b_ada, w_ff_gate, w_ff_up, w_ff_down, w_in, ml_conv_w, ml_conv_b, ml_b_i, ml_b_f, ml_norm_w, sb_q_norm_w, sb_k_norm_w, hg_lb_logits, hg_norm_w, w_branch, w_merge_gate, b_merge_gate, w_out):
    depth = w_in.shape[0]
    D = x_prompt.shape[-1]
    Bp, Bs = x_prompt.shape[0], x_sample.shape[0]
    lb_cum = jnp.cumsum(jax.nn.softmax(hg_lb_logits.astype(F32), axis=0), axis=0)
    lb_all = lb_cum - lb_cum[0:1]
    ada = _ada(jnp.concatenate([c_prompt, c_sample], axis=0), w_ada, b_ada)
    ada = ada.reshape(depth, Bp + Bs, 9, 1, D)

    past_p = (None, None, jnp.zeros((Bp, N_HEADS, HEAD_DIM, HEAD_DIM), F32), jnp.zeros((Bp, N_HEADS, HEAD_DIM), F32),
              jnp.zeros((Bp, N_HEADS), F32), jnp.zeros((Bp, CONV_W - 1, 2 * WIDTH), F32),
              jnp.zeros((Bp, N_HEADS, HEAD_DIM, HEAD_DIM), F32))
    xp, xs = x_prompt, x_sample
    new_p, new_s = [], []
    for l in range(depth):
        lp = _layer_params(l, lb_all, norm_w, w_ff_gate, w_ff_up, w_ff_down, w_in, ml_conv_w, ml_conv_b, ml_b_i,
                           ml_b_f, ml_norm_w, sb_q_norm_w, sb_k_norm_w, hg_norm_w, w_branch, w_merge_gate,
                           b_merge_gate, w_out)
        xp, st_p = _layer(xp, ada[l, :Bp], past_p, lp)
        past_s = (cache_sb_k[l], cache_sb_v[l], state_mlstm_C[l], state_mlstm_n[l], state_mlstm_m[l],
                  state_mlstm_conv[l], state_hgrn_S[l])
        xs, st_s = _layer(xs, ada[l, Bp:], past_s, lp)
        new_p.append(st_p)
        new_s.append(st_s)

    def stk(states, i):
        return jnp.stack([s[i] for s in states], axis=0)

    return (xp, xs, stk(new_p, 0), stk(new_s, 0), stk(new_p, 1), stk(new_s, 1), stk(new_p, 2), stk(new_s, 2),
            stk(new_p, 3), stk(new_s, 3), stk(new_p, 4), stk(new_s, 4), stk(new_p, 5), stk(new_s, 5),
            stk(new_p, 6), stk(new_s, 6))
```
